```python
import math
import jax, jax.numpy as jnp
from jax import lax
import numpy as np

D_MODEL = 2048
BATCH = 2
SEQ = 4096
DEPTH = 1

D_MIX = D_MODEL
D_CONV = D_MIX // 2
CONV_GROUPS = 16
CONV_WIDTH = 3
D_MLSTM = D_MIX - D_CONV
MLSTM_HEADS = 4
MLSTM_HEAD_DIM = D_MLSTM // MLSTM_HEADS
MLSTM_CHUNK = 64
N_KEYS = 128
N_EXPERTS = N_KEYS * N_KEYS
PEER_HEADS = 8
PEER_TOPK = 16
D_KEY = 256
PEER_BLOCK = 128
EPS = 1e-6

D_IN_PROJ = 3 * D_CONV + 4 * D_MLSTM + 2 * MLSTM_HEADS

kernel_name = "hymba_conv_mlstm_peer_layer"


def rmsnorm(x, g):
    xf = x.astype(jnp.float32)
    y = xf * lax.rsqrt(jnp.mean(xf * xf, axis=-1, keepdims=True) + EPS)
    return (y * g.astype(jnp.float32)).astype(x.dtype)


def short_conv_mixer(xin, gate_b, gate_c, conv_w):
    u = gate_c * xin
    seq = u.shape[1]
    up = jnp.pad(u, ((0, 0), (CONV_WIDTH - 1, 0), (0, 0)))
    y = conv_w[0] * up[:, 0:seq]
    for j in range(1, CONV_WIDTH):
        y = y + conv_w[j] * up[:, j:j + seq]
    return gate_b * y


def mlstm_chunkwise(q, k, v, i_pre, log_f):
    bsz, seq, nh, dh = q.shape
    nc = seq // MLSTM_CHUNK
    L = MLSTM_CHUNK

    def to_chunks(t):
        return t.reshape(bsz, nc, L, nh, dh).transpose(1, 0, 3, 2, 4)

    def gate_chunks(t):
        return t.reshape(bsz, nc, L, nh).transpose(1, 0, 3, 2)

    causal = jnp.tril(jnp.ones((L, L), dtype=bool))

    def step(carry, inp):
        C, n, m = carry
        qc, kc, vc, ic, fc = inp
        b = jnp.cumsum(fc, axis=-1)
        dmat = b[..., :, None] - b[..., None, :] + ic[..., None, :]
        dmat = jnp.where(causal, dmat, -jnp.inf)
        inter = b + m[..., None]
        m_loc = jnp.maximum(inter, jnp.max(dmat, axis=-1))
        sw = jnp.einsum('bhjd,bhld->bhjl', qc, kc) * jnp.exp(dmat - m_loc[..., None])
        ei = jnp.exp(inter - m_loc)
        num = ei[..., None] * jnp.einsum('bhvd,bhjd->bhjv', C, qc) + jnp.einsum('bhjl,bhlv->bhjv', sw, vc)
        nq = ei * jnp.einsum('bhd,bhjd->bhj', n, qc) + jnp.sum(sw, axis=-1)
        h = num / jnp.maximum(jnp.abs(nq), jnp.exp(-m_loc))[..., None]
        b_last = b[..., -1]
        g = b_last[..., None] - b + ic
        m_new = jnp.maximum(b_last + m, jnp.max(g, axis=-1))
        decay = jnp.exp(b_last + m - m_new)
        w = jnp.exp(g - m_new[..., None])
        C_new = decay[..., None, None] * C + jnp.einsum('bhl,bhlv,bhld->bhvd', w, vc, kc)
        n_new = decay[..., None] * n + jnp.einsum('bhl,bhld->bhd', w, kc)
        return (C_new, n_new, m_new), h

    init = (jnp.zeros((bsz, nh, dh, dh), jnp.float32),
            jnp.zeros((bsz, nh, dh), jnp.float32),
            jnp.zeros((bsz, nh), jnp.float32))
    _, h = lax.scan(step, init, (to_chunks(q), to_chunks(k), to_chunks(v),
                                 gate_chunks(i_pre), gate_chunks(log_f)))
    return h.transpose(1, 0, 3, 2, 4).reshape(bsz, seq, nh, dh)


def peer_ffn(xn, w_query, sub_keys_1, sub_keys_2, expert_u, expert_v):
    bsz, seq, d = xn.shape
    t = bsz * seq
    xf = xn.reshape(t, d)
    q = (xf @ w_query).reshape(t, PEER_HEADS, D_KEY)
    q1, q2 = q[..., :D_KEY // 2], q[..., D_KEY // 2:]
    s1 = jnp.einsum('thd,nd->thn', q1, sub_keys_1)
    s2 = jnp.einsum('thd,nd->thn', q2, sub_keys_2)
    v1, i1 = lax.top_k(s1, PEER_TOPK)
    v2, i2 = lax.top_k(s2, PEER_TOPK)
    cand = (v1[..., :, None] + v2[..., None, :]).reshape(t, PEER_HEADS, PEER_TOPK * PEER_TOPK)
    sc, ci = lax.top_k(cand, PEER_TOPK)
    a = ci // PEER_TOPK
    bsel = ci % PEER_TOPK
    idx = jnp.take_along_axis(i1, a, axis=-1) * N_KEYS + jnp.take_along_axis(i2, bsel, axis=-1)
    gate = jax.nn.softmax(sc.astype(jnp.float32), axis=-1).astype(xn.dtype)
    nb = t // PEER_BLOCK

    def block(args):
        xb, ib, gb = args
        u = jnp.take(expert_u, ib, axis=0)
        act = jax.nn.gelu(jnp.einsum('td,thkd->thk', xb, u), approximate=False)
        vv = jnp.take(expert_v, ib, axis=0)
        return jnp.einsum('thk,thkd->td', gb * act, vv)

    y = lax.map(block, (xf.reshape(nb, PEER_BLOCK, d),
                        idx.reshape(nb, PEER_BLOCK, PEER_HEADS, PEER_TOPK),
                        gate.reshape(nb, PEER_BLOCK, PEER_HEADS, PEER_TOPK)))
    return y.reshape(bsz, seq, d)


def setup_inputs(seed: int = 0) -> dict:
    key = jax.random.key(seed)
    ks = jax.random.split(key, 16)
    f32 = jnp.float32
    nrm = lambda k, shape, s: jax.random.normal(k, shape, f32) * s
    x = jax.random.normal(ks[0], (BATCH, SEQ, D_MODEL), f32)
    norm_mix_g = 1.0 + nrm(ks[1], (DEPTH, D_MODEL), 0.02)
    w_in = nrm(ks[2], (DEPTH, D_MODEL, D_IN_PROJ), D_MODEL ** -0.5)
    b_igate = nrm(ks[3], (DEPTH, MLSTM_HEADS), 0.1)
    b_fgate = jnp.linspace(3.0, 6.0, MLSTM_HEADS, dtype=f32)[None, :] + nrm(ks[4], (DEPTH, MLSTM_HEADS), 0.1)
    conv_w = nrm(ks[5], (DEPTH, CONV_WIDTH, D_CONV), CONV_WIDTH ** -0.5)
    mlstm_norm_g = 1.0 + nrm(ks[6], (DEPTH, D_MLSTM), 0.02)
    w_out = nrm(ks[7], (DEPTH, D_MIX, D_MODEL), D_MIX ** -0.5)
    norm_ffn_g = 1.0 + nrm(ks[8], (DEPTH, D_MODEL), 0.02)
    w_query = nrm(ks[9], (DEPTH, D_MODEL, PEER_HEADS * D_KEY), D_MODEL ** -0.5)
    sub_keys_1 = nrm(ks[10], (DEPTH, N_KEYS, D_KEY // 2), (D_KEY // 2) ** -0.5)
    sub_keys_2 = nrm(ks[11], (DEPTH, N_KEYS, D_KEY // 2), (D_KEY // 2) ** -0.5)
    expert_u = nrm(ks[12], (DEPTH, N_EXPERTS, D_MODEL), D_MODEL ** -0.5)
    expert_v = nrm(ks[13], (DEPTH, N_EXPERTS, D_MODEL), PEER_HEADS ** -0.5)
    norm_final_g = 1.0 + nrm(ks[14], (D_MODEL,), 0.02)
    return {"x": x, "norm_mix_g": norm_mix_g, "w_in": w_in, "b_igate": b_igate,
            "b_fgate": b_fgate, "conv_w": conv_w, "mlstm_norm_g": mlstm_norm_g,
            "w_out": w_out, "norm_ffn_g": norm_ffn_g, "w_query": w_query,
            "sub_keys_1": sub_keys_1, "sub_keys_2": sub_keys_2, "expert_u": expert_u,
            "expert_v": expert_v, "norm_final_g": norm_final_g}


def reference(x, norm_mix_g, w_in, b_igate, b_fgate, conv_w, mlstm_norm_g, w_out,
              norm_ffn_g, w_query, sub_keys_1, sub_keys_2, expert_u, expert_v, norm_final_g):
    bsz, seq, _ = x.shape
    split_points = [D_CONV, 2 * D_CONV, 3 * D_CONV,
                    3 * D_CONV + D_MLSTM, 3 * D_CONV + 2 * D_MLSTM,
                    3 * D_CONV + 3 * D_MLSTM, 3 * D_CONV + 4 * D_MLSTM,
                    3 * D_CONV + 4 * D_MLSTM + MLSTM_HEADS]
    for layer in range(DEPTH):
        xn = rmsnorm(x, norm_mix_g[layer])
        proj = xn @ w_in[layer]
        cx, cb, cc, q, k, v, o, ig, fg = jnp.split(proj, split_points, axis=-1)
        conv_out = short_conv_mixer(cx, cb, cc, conv_w[layer])

        shp = (bsz, seq, MLSTM_HEADS, MLSTM_HEAD_DIM)
        qf = q.reshape(shp).astype(jnp.float32)
        kf = k.reshape(shp).astype(jnp.float32) * (MLSTM_HEAD_DIM ** -0.5)
        vf = v.reshape(shp).astype(jnp.float32)
        i_pre = (ig + b_igate[layer]).astype(jnp.float32)
        log_f = jax.nn.log_sigmoid((fg + b_fgate[layer]).astype(jnp.float32))
        h = mlstm_chunkwise(qf, kf, vf, i_pre, log_f)
        h = h * lax.rsqrt(jnp.mean(h * h, axis=-1, keepdims=True) + EPS)
        h = h.reshape(bsz, seq, D_MLSTM) * mlstm_norm_g[layer].astype(jnp.float32)
        mlstm_out = (jax.nn.sigmoid(o.astype(jnp.float32)) * h).astype(x.dtype)

        mix = jnp.concatenate([conv_out, mlstm_out], axis=-1) @ w_out[layer]
        x = x + mix
        xn2 = rmsnorm(x, norm_ffn_g[layer])
        x = x + peer_ffn(xn2, w_query[layer], sub_keys_1[layer], sub_keys_2[layer],
                         expert_u[layer], expert_v[layer])
    return rmsnorm(x, norm_final_g)
```

```python
import functools
import math

import jax
import jax.numpy as jnp
from jax import lax
from jax.experimental import pallas as pl
from jax.experimental.pallas import tpu as pltpu

F32 = jnp.float32
BF16 = jnp.bfloat16

EPS = 1e-6
D_CONV = 1024
CONV_WIDTH = 3
D_MLSTM = 1024
MLSTM_HEADS = 4
MLSTM_HEAD_DIM = 256
N_KEYS = 128
PEER_HEADS = 8
PEER_TOPK = 16
D_KEY = 256
LANES = 128
VMEM_LIMIT = 56 * 1024 * 1024

NT_DIMS = (((1,), (1,)), ((), ()))
TN_DIMS = (((0,), (0,)), ((), ()))


def _params(sem):
    return pltpu.CompilerParams(dimension_semantics=sem, vmem_limit_bytes=VMEM_LIMIT)


def _rms(x, g):
    return x * lax.rsqrt(jnp.mean(x * x, axis=-1, keepdims=True) + EPS) * g


def _inproj_kernel(x_ref, g_ref, w_ref, wg_ref, wgt_ref, proj_ref, gate_ref, gatet_ref, xn_scr):
    @pl.when(pl.program_id(1) == 0)
    def _():
        xnb = _rms(x_ref[...], g_ref[...]).astype(BF16)
        xn_scr[...] = xnb
        gate_ref[...] = jnp.dot(xnb, wg_ref[...], preferred_element_type=F32)
        gatet_ref[...] = lax.dot_general(wgt_ref[...], xnb, NT_DIMS, preferred_element_type=F32)

    proj_ref[...] = jnp.dot(xn_scr[...], w_ref[...], preferred_element_type=F32)


def _inproj(x2, g, w_bf, wg, wgt, tm, tn, n_main):
    t, d = x2.shape
    return pl.pallas_call(
        _inproj_kernel,
        grid=(t // tm, n_main // tn),
        in_specs=[
            pl.BlockSpec((tm, d), lambda i, j: (i, 0)),
            pl.BlockSpec((1, d), lambda i, j: (0, 0)),
            pl.BlockSpec((d, tn), lambda i, j: (0, j)),
            pl.BlockSpec((d, LANES), lambda i, j: (0, 0)),
            pl.BlockSpec((8, d), lambda i, j: (0, 0)),
        ],
        out_specs=[
            pl.BlockSpec((tm, tn), lambda i, j: (i, j)),
            pl.BlockSpec((tm, LANES), lambda i, j: (i, 0)),
            pl.BlockSpec((8, tm), lambda i, j: (0, i)),
        ],
        out_shape=[
            jax.ShapeDtypeStruct((t, n_main), F32),
            jax.ShapeDtypeStruct((t, LANES), F32),
            jax.ShapeDtypeStruct((8, t), F32),
        ],
        scratch_shapes=[pltpu.VMEM((tm, d), BF16)],
        compiler_params=_params(("parallel", "arbitrary")),
        name="inproj",
    )(x2, g, w_bf, wg, wgt)


def _conv_kernel(cx_ref, cb_ref, cc_ref, w_ref, o_ref, *, rows):
    seq = cx_ref.shape[0]
    w = w_ref[...]
    w0, w1, w2 = w[0:1, :], w[1:2, :], w[2:3, :]
    for r0 in range(0, seq, rows):
        u = cc_ref[r0:r0 + rows, :] * cx_ref[r0:r0 + rows, :]
        if r0 == 0:
            row = lax.broadcasted_iota(jnp.int32, u.shape, 0)
            u1 = jnp.where(row >= 1, pltpu.roll(u, 1, axis=0), 0.0)
            u2 = jnp.where(row >= 2, pltpu.roll(u, 2, axis=0), 0.0)
        else:
            u1 = cc_ref[r0 - 1:r0 - 1 + rows, :] * cx_ref[r0 - 1:r0 - 1 + rows, :]
            u2 = cc_ref[r0 - 2:r0 - 2 + rows, :] * cx_ref[r0 - 2:r0 - 2 + rows, :]
        y = w0 * u2 + w1 * u1 + w2 * u
        o_ref[r0:r0 + rows, :] = (cb_ref[r0:r0 + rows, :] * y).astype(o_ref.dtype)


def _conv(proj3, conv_w, cblk, rows):
    bsz, seq, _ = proj3.shape
    ncb = D_CONV // cblk
    return pl.pallas_call(
        functools.partial(_conv_kernel, rows=rows),
        grid=(bsz, ncb),
        in_specs=[
            pl.BlockSpec((None, seq, cblk), lambda b, c: (b, 0, c)),
            pl.BlockSpec((None, seq, cblk), lambda b, c: (b, 0, ncb + c)),
            pl.BlockSpec((None, seq, cblk), lambda b, c: (b, 0, 2 * ncb + c)),
            pl.BlockSpec((CONV_WIDTH, cblk), lambda b, c: (0, c)),
        ],
        out_specs=pl.BlockSpec((None, seq, cblk), lambda b, c: (b, 0, c)),
        out_shape=jax.ShapeDtypeStruct((bsz, seq, D_CONV), BF16),
        compiler_params=_params(("parallel", "parallel")),
        name="conv",
    )(proj3, proj3, proj3, conv_w)


def _log_sigmoid(x):
    return jnp.minimum(x, 0.0) - jnp.log(1.0 + jnp.exp(-jnp.abs(x)))


def _mlstm_kernel(bias_ref, q_ref, k_ref, v_ref, o_ref, gc_ref, gr_ref, ng_ref, out_ref,
                  ct_scr, m_scr):
    h = pl.program_id(1)

    @pl.when(pl.program_id(2) == 0)
    def _():
        ct_scr[...] = jnp.zeros_like(ct_scr)
        m_scr[...] = jnp.zeros_like(m_scr)

    length = q_ref.shape[0]
    dh = q_ref.shape[1]
    b_i = bias_ref[0, h]
    b_f = bias_ref[1, h]

    gc = gc_ref[...]
    lane = lax.broadcasted_iota(jnp.int32, gc.shape, 1)
    ig_col = jnp.sum(jnp.where(lane == h, gc, 0.0), axis=1, keepdims=True) + b_i
    fg_col = jnp.sum(jnp.where(lane == h + MLSTM_HEADS, gc, 0.0), axis=1, keepdims=True) + b_f
    gr = gr_ref[...]
    sub = lax.broadcasted_iota(jnp.int32, gr.shape, 0)
    ig_row = jnp.sum(jnp.where(sub == h, gr, 0.0), axis=0, keepdims=True) + b_i
    fg_row = jnp.sum(jnp.where(sub == h + MLSTM_HEADS, gr, 0.0), axis=0, keepdims=True) + b_f
    lf_col = _log_sigmoid(fg_col)
    lf_row = _log_sigmoid(fg_row)

    rowi = lax.broadcasted_iota(jnp.int32, (length, length), 0)
    coli = lax.broadcasted_iota(jnp.int32, (length, length), 1)
    causal = coli <= rowi
    b_col = jnp.sum(jnp.where(causal, lf_row, 0.0), axis=1, keepdims=True)
    b_row = jnp.sum(jnp.where(rowi <= coli, lf_col, 0.0), axis=0, keepdims=True)
    b_last = jnp.sum(lf_row, axis=1, keepdims=True)

    m_prev = m_scr[...]
    dmat = jnp.where(causal, b_col - b_row + ig_row, -jnp.inf)
    inter = b_col + m_prev
    m_loc = jnp.maximum(inter, jnp.max(dmat, axis=1, keepdims=True))

    qb = q_ref[...].astype(BF16)
    ks = k_ref[...] * (dh ** -0.5)
    vaug = jnp.concatenate([v_ref[...].astype(BF16), jnp.ones((length, LANES), BF16)], axis=1)
    s = lax.dot_general(qb, ks.astype(BF16), NT_DIMS, preferred_element_type=F32)
    sw = s * jnp.exp(dmat - m_loc)
    ei = jnp.exp(inter - m_loc)
    ct = ct_scr[...]
    num_aug = (ei * jnp.dot(qb, ct.astype(BF16), preferred_element_type=F32)
               + jnp.dot(sw.astype(BF16), vaug, preferred_element_type=F32))
    num = num_aug[:, :dh]
    nq = num_aug[:, dh:dh + 1]
    hh = num / jnp.maximum(jnp.abs(nq), jnp.exp(-m_loc))
    hn = _rms(hh, ng_ref[...])
    out_ref[...] = (jax.nn.sigmoid(o_ref[...]) * hn).astype(out_ref.dtype)

    g_col = b_last - b_col + ig_col
    m_new = jnp.maximum(b_last + m_prev, jnp.max(g_col, axis=0, keepdims=True))
    decay = jnp.exp(b_last + m_prev - m_new)
    kw = (ks * jnp.exp(g_col - m_new)).astype(BF16)
    ct_scr[...] = decay * ct + lax.dot_general(kw, vaug, TN_DIMS, preferred_element_type=F32)
    m_scr[...] = m_new


def _mlstm(proj, gates, gates_t, bias, norm_g, bsz, seq, chunk):
    t = proj.shape[0]
    nc = seq // chunk
    dh = MLSTM_HEAD_DIM
    base = 3 * D_CONV // dh

    def col(off):
        return pl.BlockSpec((chunk, dh), lambda b, h, c: (b * nc + c, base + off * MLSTM_HEADS + h))

    return pl.pallas_call(
        _mlstm_kernel,
        grid=(bsz, MLSTM_HEADS, nc),
        in_specs=[
            pl.BlockSpec(memory_space=pltpu.SMEM),
            col(0), col(1), col(2), col(3),
            pl.BlockSpec((chunk, LANES), lambda b, h, c: (b * nc + c, 0)),
            pl.BlockSpec((8, chunk), lambda b, h, c: (0, b * nc + c)),
            pl.BlockSpec((1, dh), lambda b, h, c: (0, h)),
        ],
        out_specs=pl.BlockSpec((chunk, dh), lambda b, h, c: (b * nc + c, h)),
        out_shape=jax.ShapeDtypeStruct((t, D_MLSTM), BF16),
        scratch_shapes=[pltpu.VMEM((dh, dh + LANES), F32), pltpu.VMEM((1, 1), F32)],
        compiler_params=_params(("parallel", "parallel", "arbitrary")),
        name="mlstm",
    )(bias, proj, proj, proj, proj, gates, gates_t, norm_g)


def _outproj_kernel(x_ref, cv_ref, ml_ref, wo_ref, g_ref, x1_ref, xn_ref, xnt_ref):
    mix = (jnp.dot(cv_ref[...], wo_ref[0:D_CONV, :], preferred_element_type=F32)
           + jnp.dot(ml_ref[...], wo_ref[D_CONV:, :], preferred_element_type=F32))
    x1 = x_ref[...] + mix
    x1_ref[...] = x1
    xn = _rms(x1, g_ref[...])
    xn_ref[...] = xn.astype(BF16)
    xnt_ref[...] = xn.T.astype(BF16)


def _outproj(x2, conv_out, mlstm_out, wo_bf, g, tm):
    t, d = x2.shape
    return pl.pallas_call(
        _outproj_kernel,
        grid=(t // tm,),
        in_specs=[
            pl.BlockSpec((tm, d), lambda i: (i, 0)),
            pl.BlockSpec((tm, D_CONV), lambda i: (i, 0)),
            pl.BlockSpec((tm, D_MLSTM), lambda i: (i, 0)),
            pl.BlockSpec((D_CONV + D_MLSTM, d), lambda i: (0, 0)),
            pl.BlockSpec((1, d), lambda i: (0, 0)),
        ],
        out_specs=[
            pl.BlockSpec((tm, d), lambda i: (i, 0)),
            pl.BlockSpec((tm, d), lambda i: (i, 0)),
            pl.BlockSpec((d, tm), lambda i: (0, i)),
        ],
        out_shape=[
            jax.ShapeDtypeStruct((t, d), F32),
            jax.ShapeDtypeStruct((t, d), BF16),
            jax.ShapeDtypeStruct((d, t), BF16),
        ],
        compiler_params=_params(("parallel",)),
        name="outproj",
    )(x2, conv_out, mlstm_out, wo_bf, g)


def _extract_topk(cur_scr, rank_scr, val_scr, sel_value):
    n, width = cur_scr.shape
    rows = lax.broadcasted_iota(jnp.int32, (n, width), 0).astype(F32)
    rank_scr[...] = jnp.full((n, width), float(PEER_TOPK), F32)

    def body(r, carry):
        cur = cur_scr[...]
        m = jnp.max(cur, axis=0, keepdims=True)
        first = jnp.min(jnp.where(cur == m, rows, float(n)), axis=0, keepdims=True)
        hit = rows == first
        cur_scr[...] = jnp.where(hit, sel_value, cur)
        rank_scr[...] = jnp.where(hit, r.astype(F32), rank_scr[...])
        val_scr[pl.ds(r, 1), :] = m
        return carry

    lax.fori_loop(0, PEER_TOPK, body, 0)


def _select_kernel(xn_ref, wq_ref, k1_ref, k2_ref, r2_ref, p2_ref, cnt_ref, c1_ref,
                   cur1, rank1, v1, cur2, rank2, v2, cand, crank, cval, cnt16):
    half = D_KEY // 2
    q = jnp.dot(xn_ref[...], wq_ref[...], preferred_element_type=F32)
    s1 = lax.dot_general(k1_ref[...], q[:, :half].astype(BF16), NT_DIMS,
                         preferred_element_type=F32)
    s2 = lax.dot_general(k2_ref[...], q[:, half:].astype(BF16), NT_DIMS,
                         preferred_element_type=F32)
    cur1[...] = s1
    cur2[...] = s2
    _extract_topk(cur1, rank1, v1, -jnp.inf)
    _extract_topk(cur2, rank2, v2, -jnp.inf)

    v2v = v2[...]
    for a in range(PEER_TOPK):
        cand[a * PEER_TOPK:(a + 1) * PEER_TOPK, :] = v1[a:a + 1, :] + v2v
    top = v1[0:1, :] + v2[0:1, :]
    pexp = jnp.exp(cand[...] - top)
    _extract_topk(cand, crank, cval, -jnp.inf)
    picked = crank[...] < float(PEER_TOPK)
    z = jnp.sum(jnp.where(picked, pexp, 0.0), axis=0, keepdims=True)
    for a in range(PEER_TOPK):
        blk = jnp.where(crank[a * PEER_TOPK:(a + 1) * PEER_TOPK, :] < float(PEER_TOPK), 1.0, 0.0)
        cnt16[a:a + 1, :] = jnp.sum(blk, axis=0, keepdims=True)

    r1 = rank1[...]
    cnt = jnp.zeros_like(r1)
    for a in range(PEER_TOPK):
        cnt = jnp.where(r1 == float(a), cnt16[a:a + 1, :], cnt)
    cnt_ref[...] = cnt
    c1_ref[...] = jnp.exp(s1 - v1[0:1, :]) / z
    p2_ref[...] = jnp.exp(s2 - v2[0:1, :])
    r2_ref[...] = rank2[...]


def _select(xn2, wq_bf, k1_bf, k2_bf, tm):
    t, d = xn2.shape
    tab = jax.ShapeDtypeStruct((PEER_HEADS, N_KEYS, t), F32)
    tab_spec = pl.BlockSpec((None, N_KEYS, tm), lambda i, h: (h, 0, i))
    key_scr = pltpu.VMEM((N_KEYS, tm), F32)
    top_scr = pltpu.VMEM((PEER_TOPK, tm), F32)
    cand_scr = pltpu.VMEM((PEER_TOPK * PEER_TOPK, tm), F32)
    return pl.pallas_call(
        _select_kernel,
        grid=(t // tm, PEER_HEADS),
        in_specs=[
            pl.BlockSpec((tm, d), lambda i, h: (i, 0)),
            pl.BlockSpec((d, D_KEY), lambda i, h: (0, h)),
            pl.BlockSpec((N_KEYS, D_KEY // 2), lambda i, h: (0, 0)),
            pl.BlockSpec((N_KEYS, D_KEY // 2), lambda i, h: (0, 0)),
        ],
        out_specs=[tab_spec, tab_spec, tab_spec, tab_spec],
        out_shape=[tab, tab, tab, tab],
        scratch_shapes=[key_scr, key_scr, top_scr, key_scr, key_scr, top_scr,
                        cand_scr, cand_scr, top_scr, top_scr],
        compiler_params=_params(("parallel", "arbitrary")),
        name="select",
    )(xn2, wq_bf, k1_bf, k2_bf)


def _gelu(x):
    return 0.5 * x * (1.0 + lax.erf(x * (1.0 / math.sqrt(2.0))))


def _peer_kernel(xt_ref, u_ref, v_ref, r2_ref, p2_ref, cnt_ref, c1_ref, y_ref, p_scr, *, e1_per_blk):
    i = pl.program_id(1)

    @pl.when(i == 0)
    def _():
        y_ref[...] = jnp.zeros_like(y_ref)

    act = jnp.dot(u_ref[...], xt_ref[...], preferred_element_type=F32)
    for l in range(e1_per_blk):
        e1 = i * e1_per_blk + l
        w = None
        for h in range(PEER_HEADS):
            cnt_row = cnt_ref[h, pl.ds(e1, 1), :]
            c1_row = c1_ref[h, pl.ds(e1, 1), :]
            term = jnp.where(r2_ref[h] < cnt_row, p2_ref[h], 0.0) * c1_row
            w = term if w is None else w + term
        g = _gelu(act[l * N_KEYS:(l + 1) * N_KEYS, :])
        p_scr[l * N_KEYS:(l + 1) * N_KEYS, :] = (w * g).astype(BF16)
    y_ref[...] += lax.dot_general(p_scr[...], v_ref[...], TN_DIMS, preferred_element_type=F32)


def _peer(xn2t, u_bf, v_bf, r2, p2, cnt, c1, tm, te):
    d, t = xn2t.shape
    n_exp = u_bf.shape[0]
    tab_spec = pl.BlockSpec((PEER_HEADS, N_KEYS, tm), lambda j, i: (0, 0, j))
    return pl.pallas_call(
        functools.partial(_peer_kernel, e1_per_blk=te // N_KEYS),
        grid=(t // tm, n_exp // te),
        in_specs=[
            pl.BlockSpec((d, tm), lambda j, i: (0, j)),
            pl.BlockSpec((te, d), lambda j, i: (i, 0)),
            pl.BlockSpec((te, d), lambda j, i: (i, 0)),
            tab_spec, tab_spec, tab_spec, tab_spec,
        ],
        out_specs=pl.BlockSpec((tm, d), lambda j, i: (j, 0)),
        out_shape=jax.ShapeDtypeStruct((t, d), F32),
        scratch_shapes=[pltpu.VMEM((te, tm), BF16)],
        compiler_params=_params(("parallel", "arbitrary")),
        name="peer",
    )(xn2t, u_bf, v_bf, r2, p2, cnt, c1)


def _final_kernel(x1_ref, y_ref, g_ref, o_ref):
    o_ref[...] = _rms(x1_ref[...] + y_ref[...], g_ref[...])


def _final(x1, y, g, tm):
    t, d = x1.shape
    row = pl.BlockSpec((tm, d), lambda i: (i, 0))
    return pl.pallas_call(
        _final_kernel,
        grid=(t // tm,),
        in_specs=[row, row, pl.BlockSpec((1, d), lambda i: (0, 0))],
        out_specs=row,
        out_shape=jax.ShapeDtypeStruct((t, d), F32),
        compiler_params=_params(("parallel",)),
        name="final",
    )(x1, y, g)


def kernel(x, norm_mix_g, w_in, b_igate, b_fgate, conv_w, mlstm_norm_g, w_out, norm_ffn_g,
           w_query, sub_keys_1, sub_keys_2, expert_u, expert_v, norm_final_g):
    bsz, seq, d = x.shape
    t = bsz * seq
    depth = w_in.shape[0]
    n_main = 3 * D_CONV + 4 * D_MLSTM
    xcur = x.reshape(t, d)
    for layer in range(depth):
        w_bf = w_in[layer].astype(BF16)
        w_gate = w_in[layer][:, n_main:]
        wg = jnp.pad(w_gate, ((0, 0), (0, LANES - w_gate.shape[1]))).astype(BF16)
        wgt = w_gate.T.astype(BF16)
        proj, gates, gates_t = _inproj(xcur, norm_mix_g[layer][None, :], w_bf, wg, wgt,
                                       tm=min(512, t), tn=512, n_main=n_main)
        conv_out = _conv(proj.reshape(bsz, seq, n_main), conv_w[layer], cblk=LANES,
                         rows=min(512, seq))
        bias = jnp.stack([b_igate[layer], b_fgate[layer]])
        mlstm_out = _mlstm(proj, gates, gates_t, bias, mlstm_norm_g[layer][None, :],
                           bsz, seq, chunk=min(256, seq))
        x1, xn2, xn2t = _outproj(xcur, conv_out.reshape(t, D_CONV), mlstm_out,
                                 w_out[layer].astype(BF16), norm_ffn_g[layer][None, :],
                                 tm=min(256, t))
        r2, p2, cnt, c1 = _select(xn2, w_query[layer].astype(BF16),
                                  sub_keys_1[layer].astype(BF16), sub_keys_2[layer].astype(BF16),
                                  tm=min(256, t))
        y = _peer(xn2t, expert_u[layer].astype(BF16), expert_v[layer].astype(BF16),
                  r2, p2, cnt, c1, tm=min(512, t), te=512)
        if layer + 1 < depth:
            xcur = x1 + y
        else:
            xcur = _final(x1, y, norm_final_g[None, :], tm=min(512, t))
    return xcur.reshape(bsz, seq, d)
```
